```python
import math
import jax, jax.numpy as jnp
from jax import lax
import numpy as np

D_MODEL = 1024
BATCH = 2
SEQ = 8192
DEPTH = 4
DEC_BATCH = 1
DEC_SEQ = 16384
PAST_LEN = 128

CONV_WIDTH = 512
CONV_GROUPS = 8
CONV_K = 3
N_HEADS = 4
HEAD_DIM = 64
V_DIM = 2 * HEAD_DIM
ATTN_WIDTH = N_HEADS * V_DIM
QK_WIDTH = N_HEADS * 2 * HEAD_DIM
MIX_WIDTH = CONV_WIDTH + ATTN_WIDTH
IN_WIDTH = 3 * CONV_WIDTH + 2 * QK_WIDTH + ATTN_WIDTH
ROT_DIM = HEAD_DIM // 4
ROPE_THETA = 500000.0
Q_BLOCK = 128
N_EXPERTS = 16
EC_FACTOR = 2
EXPERT_FF = 2048
EPS = 1e-6

kernel_name = 'hymba_conv_diffattn_ec_encoder'

_SPLITS = [CONV_WIDTH, 2 * CONV_WIDTH, 3 * CONV_WIDTH,
           3 * CONV_WIDTH + QK_WIDTH, 3 * CONV_WIDTH + 2 * QK_WIDTH]


def rmsnorm(x, g):
    xf = x.astype(jnp.float32)
    y = xf * lax.rsqrt(jnp.mean(xf * xf, axis=-1, keepdims=True) + EPS)
    return (y * g.astype(jnp.float32)).astype(x.dtype)


def partial_rope(x, cos, sin):
    xr, xp = x[..., :ROT_DIM], x[..., ROT_DIM:]
    x1, x2 = xr[..., :ROT_DIM // 2], xr[..., ROT_DIM // 2:]
    rot = jnp.concatenate([x1 * cos - x2 * sin, x2 * cos + x1 * sin], axis=-1)
    return jnp.concatenate([rot, xp], axis=-1)


def diff_attention(q, k, v, lam):
    B, S = q.shape[0], q.shape[1]
    nb = S // Q_BLOCK
    qb = q.reshape(B, nb, Q_BLOCK, N_HEADS, 2, HEAD_DIM).transpose(1, 0, 2, 3, 4, 5)
    scale = HEAD_DIM ** -0.5

    def block(qi):
        s = jnp.einsum('bqhcd,bkhcd->bhcqk', qi, k).astype(jnp.float32) * scale
        p = jax.nn.softmax(s, axis=-1)
        a = p[:, :, 0] - lam * p[:, :, 1]
        return jnp.einsum('bhqk,bkhd->bqhd', a.astype(v.dtype), v)

    o = lax.map(block, qb)
    return o.transpose(1, 0, 2, 3, 4).reshape(B, S, N_HEADS, V_DIM)


def expert_choice_ffn(x, w_r, w_g, w_u, w_d):
    B, S, D = x.shape
    n = B * S
    cap = EC_FACTOR * n // N_EXPERTS
    xf = x.reshape(n, D)
    aff = jax.nn.softmax((xf @ w_r).astype(jnp.float32), axis=-1)
    gate, idx = lax.top_k(aff.T, cap)
    xe = xf[idx]
    h = jax.nn.silu(jnp.einsum('ecd,edf->ecf', xe, w_g)) * jnp.einsum('ecd,edf->ecf', xe, w_u)
    ye = jnp.einsum('ecf,efd->ecd', h, w_d) * gate[..., None].astype(x.dtype)
    out = jnp.zeros_like(xf).at[idx.reshape(-1)].add(ye.reshape(-1, D))
    return out.reshape(B, S, D)


def encoder(x, norm1_g, w_in, conv_w, lam_q1, lam_k1, lam_q2, lam_k2, subln_g,
            w_out, norm2_g, w_router, w_gate, w_up, w_down, final_g):
    B, S, _ = x.shape
    pos = jnp.arange(S, dtype=jnp.float32)
    inv_freq = ROPE_THETA ** (-jnp.arange(0, ROT_DIM, 2, dtype=jnp.float32) / ROT_DIM)
    ang = pos[:, None] * inv_freq[None, :]
    cos = jnp.cos(ang)[:, None, None, :].astype(x.dtype)
    sin = jnp.sin(ang)[:, None, None, :].astype(x.dtype)
    for l in range(DEPTH):
        lam_init = 0.8 - 0.6 * math.exp(-0.3 * l)
        h = rmsnorm(x, norm1_g[l])
        proj = h @ w_in[l]
        gb, gc, u, q, k, v = jnp.split(proj, _SPLITS, axis=-1)
        z = gc * u
        zp = jnp.pad(z, ((0, 0), (1, 1), (0, 0)))
        cw = conv_w[l]
        cz = cw[0] * zp[:, :-2] + cw[1] * zp[:, 1:-1] + cw[2] * zp[:, 2:]
        y_conv = gb * cz
        q = partial_rope(q.reshape(B, S, N_HEADS, 2, HEAD_DIM), cos, sin)
        k = partial_rope(k.reshape(B, S, N_HEADS, 2, HEAD_DIM), cos, sin)
        v = v.reshape(B, S, N_HEADS, V_DIM)
        lam = (jnp.exp(jnp.sum(lam_q1[l].astype(jnp.float32) * lam_k1[l].astype(jnp.float32)))
               - jnp.exp(jnp.sum(lam_q2[l].astype(jnp.float32) * lam_k2[l].astype(jnp.float32)))
               + lam_init)
        o = diff_attention(q, k, v, lam)
        o = rmsnorm(o, subln_g[l]) * (1.0 - lam_init)
        y_attn = o.reshape(B, S, ATTN_WIDTH)
        x = x + jnp.concatenate([y_conv, y_attn], axis=-1) @ w_out[l]
        x = x + expert_choice_ffn(rmsnorm(x, norm2_g[l]), w_router[l], w_gate[l], w_up[l], w_down[l])
    return rmsnorm(x, final_g)


def setup_inputs(seed: int = 0) -> dict:
    key = jax.random.key(seed)
    ks = jax.random.split(key, 18)
    f = jnp.float32

    def nrm(k, shape, s):
        return jax.random.normal(k, shape, f) * s

    return {
        'x_prompt': nrm(ks[0], (BATCH, SEQ, D_MODEL), 1.0),
        'x_sample': nrm(ks[1], (DEC_BATCH, DEC_SEQ, D_MODEL), 1.0),
        'norm1_g': 1.0 + nrm(ks[2], (DEPTH, D_MODEL), 0.02),
        'w_in': nrm(ks[3], (DEPTH, D_MODEL, IN_WIDTH), D_MODEL ** -0.5),
        'conv_w': nrm(ks[4], (DEPTH, CONV_K, CONV_WIDTH), CONV_K ** -0.5),
        'lam_q1': nrm(ks[5], (DEPTH, HEAD_DIM), 0.1),
        'lam_k1': nrm(ks[6], (DEPTH, HEAD_DIM), 0.1),
        'lam_q2': nrm(ks[7], (DEPTH, HEAD_DIM), 0.1),
        'lam_k2': nrm(ks[8], (DEPTH, HEAD_DIM), 0.1),
        'subln_g': 1.0 + nrm(ks[9], (DEPTH, V_DIM), 0.02),
        'w_out': nrm(ks[10], (DEPTH, MIX_WIDTH, D_MODEL), MIX_WIDTH ** -0.5),
        'norm2_g': 1.0 + nrm(ks[11], (DEPTH, D_MODEL), 0.02),
        'w_router': nrm(ks[12], (DEPTH, D_MODEL, N_EXPERTS), D_MODEL ** -0.5),
        'w_gate': nrm(ks[13], (DEPTH, N_EXPERTS, D_MODEL, EXPERT_FF), D_MODEL ** -0.5),
        'w_up': nrm(ks[14], (DEPTH, N_EXPERTS, D_MODEL, EXPERT_FF), D_MODEL ** -0.5),
        'w_down': nrm(ks[15], (DEPTH, N_EXPERTS, EXPERT_FF, D_MODEL), EXPERT_FF ** -0.5),
        'final_g': 1.0 + nrm(ks[16], (D_MODEL,), 0.02),
    }


def reference(x_prompt, x_sample, norm1_g, w_in, conv_w, lam_q1, lam_k1, lam_q2, lam_k2,
              subln_g, w_out, norm2_g, w_router, w_gate, w_up, w_down, final_g):
    y_prompt = encoder(x_prompt, norm1_g, w_in, conv_w, lam_q1, lam_k1, lam_q2, lam_k2,
                       subln_g, w_out, norm2_g, w_router, w_gate, w_up, w_down, final_g)
    y_sample = encoder(x_sample, norm1_g, w_in, conv_w, lam_q1, lam_k1, lam_q2, lam_k2,
                       subln_g, w_out, norm2_g, w_router, w_gate, w_up, w_down, final_g)
    return (y_prompt, y_sample)
```

```python
import functools
import math

import jax
import jax.numpy as jnp
from jax import lax
from jax.experimental import pallas as pl
from jax.experimental.pallas import tpu as pltpu

D_MODEL = 1024
DEPTH = 4
CONV_WIDTH = 512
N_HEADS = 4
HEAD_DIM = 64
V_DIM = 2 * HEAD_DIM
ATTN_WIDTH = N_HEADS * V_DIM
QK_WIDTH = N_HEADS * 2 * HEAD_DIM
ROT_DIM = HEAD_DIM // 4
ROT_HALF = ROT_DIM // 2
ROPE_THETA = 500000.0
N_EXPERTS = 16
EC_FACTOR = 2
EXPERT_FF = 2048
EPS = 1e-6

LANES = 128
BF16_SUBLANES = 16
EXT_WIDTH = D_MODEL + LANES
TOK_HI_LANE = N_EXPERTS
TOK_LO_LANE = N_EXPERTS + 1
VMEM_LIMIT = 56 * 1024 * 1024

PROJ_TILE = 512
MIX_TILE = 512
ATTN_TQ = 256
ATTN_TK = 512
FFN_TM = 512
COMBINE_TILE = 256
COMBINE_BLK = 256

_NT = (((1,), (1,)), ((), ()))
_TN = (((0,), (0,)), ((), ()))


def _params(*sem):
    return pltpu.CompilerParams(dimension_semantics=sem, vmem_limit_bytes=VMEM_LIMIT)


def _rms(xf, g):
    return xf * lax.rsqrt(jnp.mean(xf * xf, axis=-1, keepdims=True) + EPS) * g


def _proj_kernel(x_ref, g_ref, wn_ref, wt_ref, ck_ref, sk1_ref, sk2_ref, cq_ref, sq_ref,
                 gb_ref, z_ref, k_ref, qt_ref, vt_ref):
    h = _rms(x_ref[...], g_ref[...]).astype(jnp.bfloat16)
    pn = jnp.dot(h, wn_ref[...], preferred_element_type=jnp.float32)
    gb_ref[...] = pn[:, :CONV_WIDTH].astype(jnp.bfloat16)
    z_ref[...] = (pn[:, CONV_WIDTH:2 * CONV_WIDTH] * pn[:, 2 * CONV_WIDTH:3 * CONV_WIDTH]).astype(jnp.bfloat16)
    ck, sk1, sk2 = ck_ref[...], sk1_ref[...], sk2_ref[...]
    for c in range(QK_WIDTH // LANES):
        kb = pn[:, 3 * CONV_WIDTH + c * LANES:3 * CONV_WIDTH + (c + 1) * LANES]
        kr = (kb * ck + pltpu.roll(kb, LANES - ROT_HALF, axis=1) * sk1
              + pltpu.roll(kb, ROT_HALF, axis=1) * sk2)
        k_ref[:, c * LANES:(c + 1) * LANES] = kr.astype(jnp.bfloat16)
    pt = lax.dot_general(wt_ref[...], h, _NT, preferred_element_type=jnp.float32)
    cq, sq = cq_ref[...], sq_ref[...]
    pieces = []
    for g in range(QK_WIDTH // HEAD_DIM):
        base = g * HEAD_DIM
        r0 = pt[base:base + ROT_HALF]
        r1 = pt[base + ROT_HALF:base + ROT_DIM]
        pieces += [r0 * cq - r1 * sq, r1 * cq + r0 * sq, pt[base + ROT_DIM:base + HEAD_DIM]]
    qt_ref[0] = jnp.concatenate(pieces, axis=0).astype(jnp.bfloat16)
    vt_ref[0] = pt[QK_WIDTH:].astype(jnp.bfloat16)


def _proj(x2d, g, wn, wt, rope, batch, seq):
    n = x2d.shape[0]
    t = min(PROJ_TILE, seq)
    spt = seq // t
    ck, sk1, sk2, cq, sq = rope
    tok = lambda i: (i, 0)
    const = lambda i: (0, 0)
    pos = lambda i: (i % spt, 0)
    post = lambda i: (0, i % spt)
    tr = lambda i: (i // spt, 0, i % spt)
    return pl.pallas_call(
        _proj_kernel,
        grid=(n // t,),
        in_specs=[
            pl.BlockSpec((t, D_MODEL), tok),
            pl.BlockSpec((1, D_MODEL), const),
            pl.BlockSpec(wn.shape, const),
            pl.BlockSpec(wt.shape, const),
            pl.BlockSpec((t, LANES), pos),
            pl.BlockSpec((t, LANES), pos),
            pl.BlockSpec((t, LANES), pos),
            pl.BlockSpec((ROT_HALF, t), post),
            pl.BlockSpec((ROT_HALF, t), post),
        ],
        out_specs=[
            pl.BlockSpec((t, CONV_WIDTH), tok),
            pl.BlockSpec((t, CONV_WIDTH), tok),
            pl.BlockSpec((t, QK_WIDTH), tok),
            pl.BlockSpec((1, QK_WIDTH, t), tr),
            pl.BlockSpec((1, ATTN_WIDTH, t), tr),
        ],
        out_shape=[
            jax.ShapeDtypeStruct((n, CONV_WIDTH), jnp.bfloat16),
            jax.ShapeDtypeStruct((n, CONV_WIDTH), jnp.bfloat16),
            jax.ShapeDtypeStruct((n, QK_WIDTH), jnp.bfloat16),
            jax.ShapeDtypeStruct((batch, QK_WIDTH, seq), jnp.bfloat16),
            jax.ShapeDtypeStruct((batch, ATTN_WIDTH, seq), jnp.bfloat16),
        ],
        compiler_params=_params("arbitrary"),
        name="proj",
    )(x2d, g, wn, wt, ck, sk1, sk2, cq, sq)


def _attn_kernel(lam_init, tk, qt_ref, k_ref, vt_ref, lq1_ref, lk1_ref, lq2_ref, lk2_ref, sg_ref, o_ref):
    tq = qt_ref.shape[2]
    seq = k_ref.shape[1]
    qt = qt_ref[0]
    row = lax.broadcasted_iota(jnp.int32, qt.shape, 0)
    zero = jnp.zeros_like(qt)
    w = (jnp.where(row < HEAD_DIM, qt, zero), jnp.where(row >= HEAD_DIM, qt, zero))

    def body(i, carry):
        off = pl.multiple_of(i * tk, tk)
        kb = k_ref[0, pl.ds(off, tk), :]
        vb = vt_ref[0, :, pl.ds(off, tk)]
        out = []
        for c in range(2):
            m, l, acc = carry[c]
            s = jnp.dot(kb, w[c], preferred_element_type=jnp.float32)
            mn = jnp.maximum(m, jnp.max(s, axis=0, keepdims=True))
            alpha = jnp.exp(m - mn)
            p = jnp.exp(s - mn)
            l = alpha * l + jnp.sum(p, axis=0, keepdims=True)
            acc = alpha * acc + jnp.dot(vb, p.astype(jnp.bfloat16), preferred_element_type=jnp.float32)
            out.append((mn, l, acc))
        return tuple(out)

    init = tuple((jnp.full((1, tq), -jnp.inf, jnp.float32), jnp.zeros((1, tq), jnp.float32),
                  jnp.zeros((V_DIM, tq), jnp.float32)) for _ in range(2))
    (_, l1, a1), (_, l2, a2) = lax.fori_loop(0, seq // tk, body, init)
    lam = (jnp.exp(jnp.sum(lq1_ref[...] * lk1_ref[...], axis=-1, keepdims=True))
           - jnp.exp(jnp.sum(lq2_ref[...] * lk2_ref[...], axis=-1, keepdims=True)) + lam_init)
    o = a1 / l1 - lam * (a2 / l2)
    on = o * lax.rsqrt(jnp.mean(o * o, axis=0, keepdims=True) + EPS) * sg_ref[...]
    o_ref[0] = (on * (1.0 - lam_init)).T.astype(jnp.bfloat16)


def _attention(qt, k3, vt, lq1, lk1, lq2, lk2, sg, lam_init):
    batch, _, seq = qt.shape
    tq = min(ATTN_TQ, seq)
    tk = min(ATTN_TK, seq)
    vec = lambda b, h, i: (0, 0)
    return pl.pallas_call(
        functools.partial(_attn_kernel, lam_init, tk),
        grid=(batch, N_HEADS, seq // tq),
        in_specs=[
            pl.BlockSpec((1, V_DIM, tq), lambda b, h, i: (b, h, i)),
            pl.BlockSpec((1, seq, 2 * HEAD_DIM), lambda b, h, i: (b, 0, h)),
            pl.BlockSpec((1, V_DIM, seq), lambda b, h, i: (b, h, 0)),
            pl.BlockSpec((1, HEAD_DIM), vec),
            pl.BlockSpec((1, HEAD_DIM), vec),
            pl.BlockSpec((1, HEAD_DIM), vec),
            pl.BlockSpec((1, HEAD_DIM), vec),
            pl.BlockSpec((V_DIM, 1), vec),
        ],
        out_specs=pl.BlockSpec((1, tq, V_DIM), lambda b, h, i: (b, i, h)),
        out_shape=jax.ShapeDtypeStruct((batch, seq, ATTN_WIDTH), jnp.bfloat16),
        compiler_params=_params("arbitrary", "arbitrary", "arbitrary"),
        name="diff_attn",
    )(qt, k3, vt, lq1, lk1, lq2, lk2, sg)


def _mix_kernel(tiles_per_seq, x_ref, gb_ref, z_ref, zp_ref, zn_ref, ya_ref, cw_ref, wo_ref, g2_ref,
                wr_ref, wrt_ref, x1_ref, ext_ref, afft_ref):
    i = pl.program_id(0)
    t = x_ref.shape[0]
    z = z_ref[...].astype(jnp.float32)
    row = lax.broadcasted_iota(jnp.int32, z.shape, 0)
    first = (i % tiles_per_seq) == 0
    last = (i % tiles_per_seq) == tiles_per_seq - 1
    halo_p = jnp.where(first, 0.0, zp_ref[BF16_SUBLANES - 1:BF16_SUBLANES, :].astype(jnp.float32))
    halo_n = jnp.where(last, 0.0, zn_ref[0:1, :].astype(jnp.float32))
    zprev = jnp.where(row == 0, halo_p, pltpu.roll(z, 1, axis=0))
    znext = jnp.where(row == t - 1, halo_n, pltpu.roll(z, t - 1, axis=0))
    cw = cw_ref[...]
    cz = cw[0:1] * zprev + cw[1:2] * z + cw[2:3] * znext
    yc = (gb_ref[...].astype(jnp.float32) * cz).astype(jnp.bfloat16)
    x1 = (x_ref[...]
          + jnp.dot(yc, wo_ref[:CONV_WIDTH, :], preferred_element_type=jnp.float32)
          + jnp.dot(ya_ref[...], wo_ref[CONV_WIDTH:, :], preferred_element_type=jnp.float32))
    x1_ref[...] = x1
    h2 = _rms(x1, g2_ref[...])
    hb = h2.astype(jnp.bfloat16)
    ext_ref[:, :D_MODEL] = h2
    lane = lax.broadcasted_iota(jnp.int32, (t, LANES), 1)
    lg = jnp.dot(hb, wr_ref[...], preferred_element_type=jnp.float32)
    lg = jnp.where(lane < N_EXPERTS, lg, -jnp.inf)
    ex = jnp.exp(lg - jnp.max(lg, axis=-1, keepdims=True))
    aff = ex / jnp.sum(ex, axis=-1, keepdims=True)
    tok = i * t + lax.broadcasted_iota(jnp.int32, (t, LANES), 0)
    side = jnp.where(lane == TOK_HI_LANE, lax.shift_right_logical(tok, 7).astype(jnp.float32),
                     jnp.where(lane == TOK_LO_LANE, (tok & (LANES - 1)).astype(jnp.float32), aff))
    ext_ref[:, D_MODEL:] = side
    lgt = lax.dot_general(wrt_ref[...], hb, _NT, preferred_element_type=jnp.float32)
    ext = jnp.exp(lgt - jnp.max(lgt, axis=0, keepdims=True))
    afft_ref[...] = ext / jnp.sum(ext, axis=0, keepdims=True)


def _mix(x2d, gb, z, ya, cw, wo, g2, wr, wrt, seq):
    n = x2d.shape[0]
    t = min(MIX_TILE, seq)
    tps = seq // t
    hb = t // BF16_SUBLANES
    nhb = n // BF16_SUBLANES
    tok = lambda i: (i, 0)
    const = lambda i: (0, 0)
    return pl.pallas_call(
        functools.partial(_mix_kernel, tps),
        grid=(n // t,),
        in_specs=[
            pl.BlockSpec((t, D_MODEL), tok),
            pl.BlockSpec((t, CONV_WIDTH), tok),
            pl.BlockSpec((t, CONV_WIDTH), tok),
            pl.BlockSpec((BF16_SUBLANES, CONV_WIDTH), lambda i: (jnp.maximum(i * hb - 1, 0), 0)),
            pl.BlockSpec((BF16_SUBLANES, CONV_WIDTH), lambda i: (jnp.minimum((i + 1) * hb, nhb - 1), 0)),
            pl.BlockSpec((t, ATTN_WIDTH), tok),
            pl.BlockSpec(cw.shape, const),
            pl.BlockSpec(wo.shape, const),
            pl.BlockSpec((1, D_MODEL), const),
            pl.BlockSpec(wr.shape, const),
            pl.BlockSpec(wrt.shape, const),
        ],
        out_specs=[
            pl.BlockSpec((t, D_MODEL), tok),
            pl.BlockSpec((t, EXT_WIDTH), tok),
            pl.BlockSpec((N_EXPERTS, t), lambda i: (0, i)),
        ],
        out_shape=[
            jax.ShapeDtypeStruct((n, D_MODEL), jnp.float32),
            jax.ShapeDtypeStruct((n, EXT_WIDTH), jnp.float32),
            jax.ShapeDtypeStruct((N_EXPERTS, n), jnp.float32),
        ],
        compiler_params=_params("arbitrary"),
        name="mix_out",
    )(x2d, gb, z, z, z, ya, cw, wo, g2, wr, wrt)


def _tri(n, strict, lower):
    r = lax.broadcasted_iota(jnp.int32, (n, n), 0)
    c = lax.broadcasted_iota(jnp.int32, (n, n), 1)
    if lower:
        m = (c < r) if strict else (c <= r)
    else:
        m = (r < c) if strict else (r <= c)
    return jnp.where(m, 1.0, 0.0).astype(jnp.bfloat16)


def _select_kernel(cap, aff_ref, idx_ref, off_ref):
    e, nr, _ = aff_ref.shape
    bits = pltpu.bitcast(aff_ref[...], jnp.int32)
    capf = jnp.float32(cap)

    def count(mask):
        return jnp.sum(jnp.where(mask, 1.0, 0.0), axis=(1, 2), keepdims=True)

    def search(_, lohi):
        lo, hi = lohi
        mid = lo + lax.shift_right_logical(hi - lo, 1)
        ge = count(bits >= mid) >= capf
        return jnp.where(ge, mid, lo), jnp.where(ge, hi, mid)

    lo0 = jnp.zeros((e, 1, 1), jnp.int32)
    hi0 = jnp.full((e, 1, 1), 0x7F800000, jnp.int32)
    thr, _ = lax.fori_loop(0, 31, search, (lo0, hi0))
    need = capf - count(bits > thr)

    upper = _tri(LANES, strict=False, lower=False)
    ones_ll = jnp.ones((LANES, LANES), jnp.bfloat16)
    ones_rr = jnp.ones((nr, nr), jnp.bfloat16)
    lower_rr = _tri(nr, strict=True, lower=True)
    upper_rr = _tri(nr, strict=False, lower=False)
    ones_8l = jnp.ones((8, LANES), jnp.bfloat16)
    ones_8r = jnp.ones((8, nr), jnp.bfloat16)
    c_col = lax.broadcasted_iota(jnp.int32, (cap, LANES), 0).astype(jnp.float32)
    c_row = lax.broadcasted_iota(jnp.int32, (cap, nr), 0).astype(jnp.float32)
    lane_r = lax.broadcasted_iota(jnp.int32, (cap, nr), 1).astype(jnp.float32)
    f32 = jnp.float32

    for x in range(e):
        bx = pltpu.bitcast(aff_ref[x], jnp.int32)
        eqf = jnp.where(bx == thr[x], 1.0, 0.0)
        eqm = eqf.astype(jnp.bfloat16)
        eq_in = jnp.dot(eqm, upper, preferred_element_type=f32)
        eq_rc = jnp.dot(eqm, ones_ll, preferred_element_type=f32).astype(jnp.bfloat16)
        eq_off = jnp.dot(lower_rr, eq_rc, preferred_element_type=f32)
        eq_rank = eq_in - eqf + eq_off
        tie = jnp.where(eq_rank < need[x], eqf, 0.0)
        m = jnp.where(bx > thr[x], 1.0, tie).astype(jnp.bfloat16)
        lcum = jnp.dot(m, upper, preferred_element_type=f32).astype(jnp.bfloat16)
        rc = jnp.dot(m, ones_ll, preferred_element_type=f32).astype(jnp.bfloat16)
        rc_l = lax.dot_general(ones_8l, m, _NT, preferred_element_type=f32)
        rowend_l = jnp.dot(rc_l.astype(jnp.bfloat16), upper_rr, preferred_element_type=f32)
        off_ref[x:x + 1, :] = (rowend_l - rc_l)[0:1].astype(jnp.int32)
        a = jnp.where(rowend_l[0:1] <= c_row, 1.0, 0.0).astype(jnp.bfloat16)
        rowoff = jnp.dot(a, rc, preferred_element_type=f32)
        rowof = jnp.dot(a, ones_rr, preferred_element_type=f32)
        onehot = jnp.where(rowof == lane_r, 1.0, 0.0).astype(jnp.bfloat16)
        lrow = jnp.dot(onehot, lcum, preferred_element_type=f32)
        b = jnp.where(lrow <= c_col - rowoff, 1.0, 0.0).astype(jnp.bfloat16)
        col_l = lax.dot_general(ones_8l, b, _NT, preferred_element_type=f32)
        row_l = lax.dot_general(ones_8r, a, _NT, preferred_element_type=f32)
        idx_ref[x:x + 1, :] = (row_l * LANES + col_l)[0:1].astype(jnp.int32)


def _select(aff3, cap):
    e, nr, _ = aff3.shape
    return pl.pallas_call(
        functools.partial(_select_kernel, cap),
        out_shape=[jax.ShapeDtypeStruct((e, cap), jnp.int32), jax.ShapeDtypeStruct((e, nr), jnp.int32)],
        compiler_params=pltpu.CompilerParams(vmem_limit_bytes=VMEM_LIMIT),
        name="ec_select",
    )(aff3)


def _ffn_kernel(idx_ref, ext_hbm, wg_ref, wu_ref, wd_ref, y_ref, xbuf, sem):
    tm = xbuf.shape[0]
    e = pl.program_id(0)

    def issue(r, _):
        tok = idx_ref[0, 0, r]
        pltpu.make_async_copy(ext_hbm.at[pl.ds(tok, 1)], xbuf.at[pl.ds(r, 1)], sem).start()
        return 0

    lax.fori_loop(0, tm, issue, 0)

    def wait(r, _):
        pltpu.make_async_copy(ext_hbm.at[pl.ds(0, 1)], xbuf.at[pl.ds(r, 1)], sem).wait()
        return 0

    lax.fori_loop(0, tm, wait, 0)
    xb = xbuf[:, :D_MODEL].astype(jnp.bfloat16)
    side = xbuf[:, D_MODEL:]
    lane = lax.broadcasted_iota(jnp.int32, side.shape, 1)
    gate = jnp.sum(jnp.where(lane == e, side, 0.0), axis=-1, keepdims=True)
    hg = jnp.dot(xb, wg_ref[0], preferred_element_type=jnp.float32)
    hu = jnp.dot(xb, wu_ref[0], preferred_element_type=jnp.float32)
    hh = (hg * jax.nn.sigmoid(hg) * hu).astype(jnp.bfloat16)
    y = jnp.dot(hh, wd_ref[0], preferred_element_type=jnp.float32) * gate
    y_ref[:, :D_MODEL] = y.astype(jnp.bfloat16)
    keep = (lane == TOK_HI_LANE) | (lane == TOK_LO_LANE)
    y_ref[:, D_MODEL:] = jnp.where(keep, side, 0.0).astype(jnp.bfloat16)


def _ffn(idx, ext, wg, wu, wd):
    e, cap = idx.shape
    tm = min(FFN_TM, cap)
    steps = cap // tm
    idx3 = idx.reshape(e * steps, 1, tm)
    return pl.pallas_call(
        _ffn_kernel,
        grid=(e, steps),
        in_specs=[
            pl.BlockSpec((1, 1, tm), lambda x, j: (x * steps + j, 0, 0), memory_space=pltpu.SMEM),
            pl.BlockSpec(memory_space=pl.ANY),
            pl.BlockSpec((1, D_MODEL, EXPERT_FF), lambda x, j: (x, 0, 0)),
            pl.BlockSpec((1, D_MODEL, EXPERT_FF), lambda x, j: (x, 0, 0)),
            pl.BlockSpec((1, EXPERT_FF, D_MODEL), lambda x, j: (x, 0, 0)),
        ],
        out_specs=pl.BlockSpec((tm, EXT_WIDTH), lambda x, j: (x * steps + j, 0)),
        out_shape=jax.ShapeDtypeStruct((e * cap, EXT_WIDTH), jnp.bfloat16),
        scratch_shapes=[pltpu.VMEM((tm, EXT_WIDTH), jnp.float32), pltpu.SemaphoreType.DMA],
        compiler_params=_params("arbitrary", "arbitrary"),
        name="expert_ffn",
    )(idx3, ext, wg, wu, wd)


def _combine_kernel(cap, final, off_ref, x1_ref, y_hbm, fg_ref, o_ref, rbuf, sem):
    j = pl.program_id(0)
    t = x1_ref.shape[0]
    rows_per_tile = t // LANES
    chunk = BF16_SUBLANES

    @pl.when(j == 0)
    def _():
        rbuf[...] = jnp.zeros_like(rbuf)

    def chunk_copy(src_row, dst_row):
        return pltpu.make_async_copy(y_hbm.at[pl.ds(src_row, chunk)], rbuf.at[pl.ds(dst_row, chunk)], sem)

    pos = 0
    for x in range(N_EXPERTS):
        s = off_ref[x, j * rows_per_tile]
        end = off_ref[x, (j + 1) * rows_per_tile]
        a = (s // chunk) * chunk
        nch = jnp.where(end > s, (end - a + chunk - 1) // chunk, 0)

        def issue(c, _, a=a, pos=pos, x=x):
            chunk_copy(pl.multiple_of(x * cap + a + c * chunk, chunk),
                       pl.multiple_of(pos + c * chunk, chunk)).start()
            return 0

        lax.fori_loop(0, nch, issue, 0)
        pos = pos + nch * chunk

    def wait(c, _):
        chunk_copy(0, 0).wait()
        return 0

    lax.fori_loop(0, pos // chunk, wait, 0)

    base = (j * t).astype(jnp.float32)
    lane_t = lax.broadcasted_iota(jnp.int32, (COMBINE_BLK, t), 1).astype(jnp.float32)
    row_b = lax.broadcasted_iota(jnp.int32, (COMBINE_BLK, 1), 0)

    def accumulate(b, acc):
        r0 = pl.multiple_of(b * COMBINE_BLK, COMBINE_BLK)
        rb = rbuf[pl.ds(r0, COMBINE_BLK), :]
        tok = (rb[:, D_MODEL + TOK_HI_LANE:D_MODEL + TOK_HI_LANE + 1].astype(jnp.float32) * LANES
               + rb[:, D_MODEL + TOK_LO_LANE:D_MODEL + TOK_LO_LANE + 1].astype(jnp.float32))
        tok = jnp.where(row_b + r0 < pos, tok - base, -1.0)
        pt = jnp.where(tok == lane_t, 1.0, 0.0).astype(jnp.bfloat16)
        return acc + lax.dot_general(pt, rb[:, :D_MODEL], _TN, preferred_element_type=jnp.float32)

    nblk = (pos + COMBINE_BLK - 1) // COMBINE_BLK
    acc = lax.fori_loop(0, nblk, accumulate, jnp.zeros((t, D_MODEL), jnp.float32))
    x2 = x1_ref[...] + acc
    if final:
        x2 = _rms(x2, fg_ref[...])
    o_ref[...] = x2


def _combine(offs, x1, yext, fg, cap, final):
    n = x1.shape[0]
    t = min(COMBINE_TILE, n)
    rmax = N_EXPERTS * (t + 2 * BF16_SUBLANES)
    rmax = ((rmax + COMBINE_BLK - 1) // COMBINE_BLK) * COMBINE_BLK
    return pl.pallas_call(
        functools.partial(_combine_kernel, cap, final),
        grid_spec=pltpu.PrefetchScalarGridSpec(
            num_scalar_prefetch=1,
            grid=(n // t,),
            in_specs=[
                pl.BlockSpec((t, D_MODEL), lambda j, off: (j, 0)),
                pl.BlockSpec(memory_space=pl.ANY),
                pl.BlockSpec((1, D_MODEL), lambda j, off: (0, 0)),
            ],
            out_specs=pl.BlockSpec((t, D_MODEL), lambda j, off: (j, 0)),
            scratch_shapes=[pltpu.VMEM((rmax, EXT_WIDTH), jnp.bfloat16), pltpu.SemaphoreType.DMA],
        ),
        out_shape=jax.ShapeDtypeStruct((n, D_MODEL), jnp.float32),
        compiler_params=_params("arbitrary"),
        name="ec_combine",
    )(offs, x1, yext, fg)


def _rope_tables(seq):
    pos = jnp.arange(seq, dtype=jnp.float32)
    inv_freq = ROPE_THETA ** (-jnp.arange(0, ROT_DIM, 2, dtype=jnp.float32) / ROT_DIM)
    ang = pos[:, None] * inv_freq[None, :]
    cos, sin = jnp.cos(ang), jnp.sin(ang)
    ones = jnp.ones((seq, HEAD_DIM - ROT_DIM), jnp.float32)
    zeros = jnp.zeros((seq, HEAD_DIM - ROT_DIM), jnp.float32)
    z8 = jnp.zeros((seq, ROT_HALF), jnp.float32)
    rep = LANES // HEAD_DIM
    ck = jnp.tile(jnp.concatenate([cos, cos, ones], axis=1), (1, rep))
    sk1 = jnp.tile(jnp.concatenate([-sin, z8, zeros], axis=1), (1, rep))
    sk2 = jnp.tile(jnp.concatenate([z8, sin, zeros], axis=1), (1, rep))
    return ck, sk1, sk2, cos.T, sin.T


def _layer_weights(l, w_in, w_out, w_router, w_gate, w_up, w_down):
    bf = jnp.bfloat16
    c3 = 3 * CONV_WIDTH
    wi = w_in[l]
    wn = jnp.concatenate([wi[:, :c3], wi[:, c3 + QK_WIDTH:c3 + 2 * QK_WIDTH]], axis=1).astype(bf)
    wq = wi[:, c3:c3 + QK_WIDTH] * (HEAD_DIM ** -0.5)
    wt = jnp.concatenate([wq, wi[:, c3 + 2 * QK_WIDTH:]], axis=1).T.astype(bf)
    wr = jnp.pad(w_router[l], ((0, 0), (0, LANES - N_EXPERTS))).astype(bf)
    wrt = w_router[l].T.astype(bf)
    return wn, wt, w_out[l].astype(bf), wr, wrt, w_gate[l].astype(bf), w_up[l].astype(bf), w_down[l].astype(bf)


def _encoder(x, weights, norm1_g, conv_w, lam_q1, lam_k1, lam_q2, lam_k2, subln_g, norm2_g, final_g):
    batch, seq, _ = x.shape
    n = batch * seq
    cap = EC_FACTOR * n // N_EXPERTS
    assert seq % LANES == 0 and cap % BF16_SUBLANES == 0
    rope = _rope_tables(seq)
    fg = final_g.reshape(1, D_MODEL)
    xf = x.reshape(n, D_MODEL)
    for l in range(DEPTH):
        lam_init = 0.8 - 0.6 * math.exp(-0.3 * l)
        wn, wt, wo, wr, wrt, wg, wu, wd = weights[l]
        gb, z, k, qt, vt = _proj(xf, norm1_g[l].reshape(1, D_MODEL), wn, wt, rope, batch, seq)
        ya = _attention(qt, k.reshape(batch, seq, QK_WIDTH), vt,
                        lam_q1[l].reshape(1, HEAD_DIM), lam_k1[l].reshape(1, HEAD_DIM),
                        lam_q2[l].reshape(1, HEAD_DIM), lam_k2[l].reshape(1, HEAD_DIM),
                        subln_g[l].reshape(V_DIM, 1), lam_init)
        x1, ext, afft = _mix(xf, gb, z, ya.reshape(n, ATTN_WIDTH), conv_w[l], wo,
                             norm2_g[l].reshape(1, D_MODEL), wr, wrt, seq)
        idx, offs = _select(afft.reshape(N_EXPERTS, n // LANES, LANES), cap)
        yext = _ffn(idx, ext, wg, wu, wd)
        offs = jnp.concatenate([offs, jnp.full((N_EXPERTS, 1), cap, jnp.int32)], axis=1)
        xf = _combine(offs, x1, yext, fg, cap, final=(l == DEPTH - 1))
    return xf.reshape(batch, seq, D_MODEL)


def kernel(x_prompt, x_sample, norm1_g, w_in, conv_w, lam_q1, lam_k1, lam_q2, lam_k2, subln_g, w_out,
           norm2_g, w_router, w_gate, w_up, w_down, final_g):
    weights = [_layer_weights(l, w_in, w_out, w_router, w_gate, w_up, w_down) for l in range(DEPTH)]
    args = (weights, norm1_g, conv_w, lam_q1, lam_k1, lam_q2, lam_k2, subln_g, norm2_g, final_g)
    return _encoder(x_prompt, *args), _encoder(x_sample, *args)
```

```python
import functools
import math

import jax
import jax.numpy as jnp
from jax import lax
from jax.experimental import pallas as pl
from jax.experimental.pallas import tpu as pltpu

D_MODEL = 1024
DEPTH = 4
CONV_WIDTH = 512
N_HEADS = 4
HEAD_DIM = 64
V_DIM = 2 * HEAD_DIM
ATTN_WIDTH = N_HEADS * V_DIM
QK_WIDTH = N_HEADS * 2 * HEAD_DIM
ROT_DIM = HEAD_DIM // 4
ROT_HALF = ROT_DIM // 2
ROPE_THETA = 500000.0
N_EXPERTS = 16
EC_FACTOR = 2
EXPERT_FF = 2048
EPS = 1e-6

LANES = 128
BF16_SUBLANES = 16
EXT_WIDTH = D_MODEL + LANES
TOK_HI_LANE = N_EXPERTS
TOK_LO_LANE = N_EXPERTS + 1
VMEM_LIMIT = 56 * 1024 * 1024

PROJ_TILE = 512
MIX_TILE = 512
ATTN_TQ = 256
ATTN_TK = 512
ATTN_HEADROOM = 60.0
ATTN_NORM_MARGIN = 1.02
FFN_TM = 512
COMBINE_TILE = 256
COMBINE_MAIN = 1024
COMBINE_BLK = 256

_NT = (((1,), (1,)), ((), ()))


def _params(*sem):
    return pltpu.CompilerParams(dimension_semantics=sem, vmem_limit_bytes=VMEM_LIMIT)


def _rms(xf, g):
    return xf * lax.rsqrt(jnp.mean(xf * xf, axis=-1, keepdims=True) + EPS) * g


def _proj_kernel(x_ref, g_ref, wn_ref, wt_ref, ck_ref, sk1_ref, sk2_ref, cq_ref, sq_ref,
                 gb_ref, z_ref, k_ref, qt_ref, vt_ref):
    h = _rms(x_ref[...], g_ref[...]).astype(jnp.bfloat16)
    pn = jnp.dot(h, wn_ref[...], preferred_element_type=jnp.float32)
    gb_ref[...] = pn[:, :CONV_WIDTH].astype(jnp.bfloat16)
    z_ref[...] = (pn[:, CONV_WIDTH:2 * CONV_WIDTH] * pn[:, 2 * CONV_WIDTH:3 * CONV_WIDTH]).astype(jnp.bfloat16)
    ck, sk1, sk2 = ck_ref[...], sk1_ref[...], sk2_ref[...]
    for c in range(QK_WIDTH // LANES):
        kb = pn[:, 3 * CONV_WIDTH + c * LANES:3 * CONV_WIDTH + (c + 1) * LANES]
        kr = (kb * ck + pltpu.roll(kb, LANES - ROT_HALF, axis=1) * sk1
              + pltpu.roll(kb, ROT_HALF, axis=1) * sk2)
        k_ref[:, c * LANES:(c + 1) * LANES] = kr.astype(jnp.bfloat16)
    pt = lax.dot_general(wt_ref[...], h, _NT, preferred_element_type=jnp.float32)
    cq, sq = cq_ref[...], sq_ref[...]
    pieces = []
    for g in range(QK_WIDTH // HEAD_DIM):
        base = g * HEAD_DIM
        r0 = pt[base:base + ROT_HALF]
        r1 = pt[base + ROT_HALF:base + ROT_DIM]
        pieces += [r0 * cq - r1 * sq, r1 * cq + r0 * sq, pt[base + ROT_DIM:base + HEAD_DIM]]
    qt_ref[0] = jnp.concatenate(pieces, axis=0).astype(jnp.bfloat16)
    vt_ref[0] = pt[QK_WIDTH:].astype(jnp.bfloat16)


def _proj(x2d, g, wn, wt, rope, batch, seq):
    n = x2d.shape[0]
    t = min(PROJ_TILE, seq)
    spt = seq // t
    ck, sk1, sk2, cq, sq = rope
    tok = lambda i: (i, 0)
    const = lambda i: (0, 0)
    pos = lambda i: (i % spt, 0)
    post = lambda i: (0, i % spt)
    tr = lambda i: (i // spt, 0, i % spt)
    return pl.pallas_call(
        _proj_kernel,
        grid=(n // t,),
        in_specs=[
            pl.BlockSpec((t, D_MODEL), tok),
            pl.BlockSpec((1, D_MODEL), const),
            pl.BlockSpec(wn.shape, const),
            pl.BlockSpec(wt.shape, const),
            pl.BlockSpec((t, LANES), pos),
            pl.BlockSpec((t, LANES), pos),
            pl.BlockSpec((t, LANES), pos),
            pl.BlockSpec((ROT_HALF, t), post),
            pl.BlockSpec((ROT_HALF, t), post),
        ],
        out_specs=[
            pl.BlockSpec((t, CONV_WIDTH), tok),
            pl.BlockSpec((t, CONV_WIDTH), tok),
            pl.BlockSpec((t, QK_WIDTH), tok),
            pl.BlockSpec((1, QK_WIDTH, t), tr),
            pl.BlockSpec((1, ATTN_WIDTH, t), tr),
        ],
        out_shape=[
            jax.ShapeDtypeStruct((n, CONV_WIDTH), jnp.bfloat16),
            jax.ShapeDtypeStruct((n, CONV_WIDTH), jnp.bfloat16),
            jax.ShapeDtypeStruct((n, QK_WIDTH), jnp.bfloat16),
            jax.ShapeDtypeStruct((batch, QK_WIDTH, seq), jnp.bfloat16),
            jax.ShapeDtypeStruct((batch, ATTN_WIDTH, seq), jnp.bfloat16),
        ],
        compiler_params=_params("arbitrary"),
        name="proj",
    )(x2d, g, wn, wt, ck, sk1, sk2, cq, sq)


def _fold8(p):
    parts = [p[r:r + 8] for r in range(0, p.shape[0], 8)]
    while len(parts) > 1:
        parts = [parts[a] + parts[a + 1] for a in range(0, len(parts), 2)]
    return parts[0]


def _attn_kernel(lam_init, tk, qt_ref, k_ref, vt_ref, lq1_ref, lk1_ref, lq2_ref, lk2_ref, sg_ref, o_ref,
                 m_sc, l_sc, acc_sc, kmax_sc, s0_sc, s1_sc, p0_sc, p1_sc):
    tq = qt_ref.shape[2]
    seq = k_ref.shape[1]
    n_tiles = seq // tk
    f32, bf16 = jnp.float32, jnp.bfloat16
    qt = qt_ref[0]
    row = lax.broadcasted_iota(jnp.int32, qt.shape, 0)
    zero = jnp.zeros_like(qt)
    w = (jnp.where(row < HEAD_DIM, qt, zero), jnp.where(row >= HEAD_DIM, qt, zero))

    @pl.when(pl.program_id(2) == 0)
    def _():
        def knorm(i, best):
            kf = k_ref[0, pl.ds(pl.multiple_of(i * tk, tk), tk), :].astype(f32)
            return jnp.maximum(best, jnp.max(jnp.sum(kf * kf, axis=-1, keepdims=True), axis=0, keepdims=True))

        best = lax.fori_loop(0, n_tiles, knorm, jnp.zeros((1, 1), f32))
        kmax_sc[0] = jnp.sqrt(best)[0, 0]

    s_bufs = (s0_sc, s1_sc)
    p_bufs = (p0_sc, p1_sc)

    def tile_off(t):
        return pl.multiple_of(t * tk, tk)

    def scores(t, slot):
        kb = k_ref[0, pl.ds(tile_off(t), tk), :]
        for c in range(2):
            s_bufs[slot][c] = jnp.dot(kb, w[c], preferred_element_type=f32)

    def weights(slot):
        for c in range(2):
            p = jnp.exp2(s_bufs[slot][c] - m_sc[c])
            l_sc[c] += _fold8(p)
            p_bufs[slot][c] = p.astype(bf16)

    def values(t, slot):
        vb = vt_ref[0, :, pl.ds(tile_off(t), tk)]
        for c in range(2):
            acc_sc[c] += jnp.dot(vb, p_bufs[slot][c], preferred_element_type=f32)

    scores(0, 0)
    excess = []
    for c in range(2):
        m = jnp.max(s0_sc[c], axis=0, keepdims=True)
        m_sc[c] = m
        l_sc[c] = jnp.zeros((8, tq), f32)
        acc_sc[c] = jnp.zeros((V_DIM, tq), f32)
        wf = w[c].astype(f32)
        qn = jnp.sqrt(jnp.sum(wf * wf, axis=0, keepdims=True))
        excess.append(qn * (kmax_sc[0] * ATTN_NORM_MARGIN) - m)
    safe = jnp.max(jnp.maximum(excess[0], excess[1])) <= ATTN_HEADROOM

    @pl.when(safe)
    def _():
        scores(1, 1)
        weights(0)

        def body(j, _):
            a = 2 * j
            scores(a + 2, 0)
            weights(1)
            values(a, 0)
            scores(a + 3, 1)
            weights(0)
            values(a + 1, 1)
            return 0

        lax.fori_loop(0, n_tiles // 2 - 1, body, 0)
        weights(1)
        values(n_tiles - 2, 0)
        values(n_tiles - 1, 1)

    @pl.when(jnp.logical_not(safe))
    def _():
        def body(t, _):
            kb = k_ref[0, pl.ds(tile_off(t), tk), :]
            vb = vt_ref[0, :, pl.ds(tile_off(t), tk)]
            for c in range(2):
                s = jnp.dot(kb, w[c], preferred_element_type=f32)
                mn = jnp.maximum(m_sc[c], jnp.max(s, axis=0, keepdims=True))
                alpha = jnp.exp2(m_sc[c] - mn)
                p = jnp.exp2(s - mn)
                m_sc[c] = mn
                l_sc[c] = alpha * l_sc[c] + _fold8(p)
                acc_sc[c] = alpha * acc_sc[c] + jnp.dot(vb, p.astype(bf16), preferred_element_type=f32)
            return 0

        lax.fori_loop(0, n_tiles, body, 0)

    l1 = jnp.sum(l_sc[0], axis=0, keepdims=True)
    l2 = jnp.sum(l_sc[1], axis=0, keepdims=True)
    a1, a2 = acc_sc[0], acc_sc[1]
    lam = (jnp.exp(jnp.sum(lq1_ref[...] * lk1_ref[...], axis=-1, keepdims=True))
           - jnp.exp(jnp.sum(lq2_ref[...] * lk2_ref[...], axis=-1, keepdims=True)) + lam_init)
    o = a1 / l1 - lam * (a2 / l2)
    on = o * lax.rsqrt(jnp.mean(o * o, axis=0, keepdims=True) + EPS) * sg_ref[...]
    o_ref[0] = (on * (1.0 - lam_init)).T.astype(jnp.bfloat16)


def _attention(qt, k3, vt, lq1, lk1, lq2, lk2, sg, lam_init):
    batch, _, seq = qt.shape
    tq = min(ATTN_TQ, seq)
    tk = min(ATTN_TK, seq // 2)
    assert seq % (2 * tk) == 0
    vec = lambda b, h, i: (0, 0)
    return pl.pallas_call(
        functools.partial(_attn_kernel, lam_init, tk),
        grid=(batch, N_HEADS, seq // tq),
        in_specs=[
            pl.BlockSpec((1, V_DIM, tq), lambda b, h, i: (b, h, i)),
            pl.BlockSpec((1, seq, 2 * HEAD_DIM), lambda b, h, i: (b, 0, h)),
            pl.BlockSpec((1, V_DIM, seq), lambda b, h, i: (b, h, 0)),
            pl.BlockSpec((1, HEAD_DIM), vec),
            pl.BlockSpec((1, HEAD_DIM), vec),
            pl.BlockSpec((1, HEAD_DIM), vec),
            pl.BlockSpec((1, HEAD_DIM), vec),
            pl.BlockSpec((V_DIM, 1), vec),
        ],
        out_specs=pl.BlockSpec((1, tq, V_DIM), lambda b, h, i: (b, i, h)),
        out_shape=jax.ShapeDtypeStruct((batch, seq, ATTN_WIDTH), jnp.bfloat16),
        scratch_shapes=[
            pltpu.VMEM((2, 1, tq), jnp.float32),
            pltpu.VMEM((2, 8, tq), jnp.float32),
            pltpu.VMEM((2, V_DIM, tq), jnp.float32),
            pltpu.SMEM((1,), jnp.float32),
            pltpu.VMEM((2, tk, tq), jnp.float32),
            pltpu.VMEM((2, tk, tq), jnp.float32),
            pltpu.VMEM((2, tk, tq), jnp.bfloat16),
            pltpu.VMEM((2, tk, tq), jnp.bfloat16),
        ],
        compiler_params=_params("arbitrary", "arbitrary", "arbitrary"),
        name="diff_attn",
    )(qt, k3, vt, lq1, lk1, lq2, lk2, sg)


def _mix_kernel(tiles_per_seq, x_ref, gb_ref, z_ref, zp_ref, zn_ref, ya_ref, cw_ref, wo_ref, g2_ref,
                wr_ref, wrt_ref, x1_ref, ext_ref, afft_ref):
    i = pl.program_id(0)
    t = x_ref.shape[0]
    z = z_ref[...].astype(jnp.float32)
    row = lax.broadcasted_iota(jnp.int32, z.shape, 0)
    first = (i % tiles_per_seq) == 0
    last = (i % tiles_per_seq) == tiles_per_seq - 1
    halo_p = jnp.where(first, 0.0, zp_ref[BF16_SUBLANES - 1:BF16_SUBLANES, :].astype(jnp.float32))
    halo_n = jnp.where(last, 0.0, zn_ref[0:1, :].astype(jnp.float32))
    zprev = jnp.where(row == 0, halo_p, pltpu.roll(z, 1, axis=0))
    znext = jnp.where(row == t - 1, halo_n, pltpu.roll(z, t - 1, axis=0))
    cw = cw_ref[...]
    cz = cw[0:1] * zprev + cw[1:2] * z + cw[2:3] * znext
    yc = (gb_ref[...].astype(jnp.float32) * cz).astype(jnp.bfloat16)
    x1 = (x_ref[...]
          + jnp.dot(yc, wo_ref[:CONV_WIDTH, :], preferred_element_type=jnp.float32)
          + jnp.dot(ya_ref[...], wo_ref[CONV_WIDTH:, :], preferred_element_type=jnp.float32))
    x1_ref[...] = x1
    h2 = _rms(x1, g2_ref[...])
    hb = h2.astype(jnp.bfloat16)
    ext_ref[:, :D_MODEL] = h2
    lane = lax.broadcasted_iota(jnp.int32, (t, LANES), 1)
    lg = jnp.dot(hb, wr_ref[...], preferred_element_type=jnp.float32)
    lg = jnp.where(lane < N_EXPERTS, lg, -jnp.inf)
    ex = jnp.exp(lg - jnp.max(lg, axis=-1, keepdims=True))
    aff = ex / jnp.sum(ex, axis=-1, keepdims=True)
    tok = i * t + lax.broadcasted_iota(jnp.int32, (t, LANES), 0)
    side = jnp.where(lane == TOK_HI_LANE, lax.shift_right_logical(tok, 7).astype(jnp.float32),
                     jnp.where(lane == TOK_LO_LANE, (tok & (LANES - 1)).astype(jnp.float32), aff))
    ext_ref[:, D_MODEL:] = side
    lgt = lax.dot_general(wrt_ref[...], hb, _NT, preferred_element_type=jnp.float32)
    ext = jnp.exp(lgt - jnp.max(lgt, axis=0, keepdims=True))
    afft_ref[...] = ext / jnp.sum(ext, axis=0, keepdims=True)


def _mix(x2d, gb, z, ya, cw, wo, g2, wr, wrt, seq):
    n = x2d.shape[0]
    t = min(MIX_TILE, seq)
    tps = seq // t
    hb = t // BF16_SUBLANES
    nhb = n // BF16_SUBLANES
    tok = lambda i: (i, 0)
    const = lambda i: (0, 0)
    return pl.pallas_call(
        functools.partial(_mix_kernel, tps),
        grid=(n // t,),
        in_specs=[
            pl.BlockSpec((t, D_MODEL), tok),
            pl.BlockSpec((t, CONV_WIDTH), tok),
            pl.BlockSpec((t, CONV_WIDTH), tok),
            pl.BlockSpec((BF16_SUBLANES, CONV_WIDTH), lambda i: (jnp.maximum(i * hb - 1, 0), 0)),
            pl.BlockSpec((BF16_SUBLANES, CONV_WIDTH), lambda i: (jnp.minimum((i + 1) * hb, nhb - 1), 0)),
            pl.BlockSpec((t, ATTN_WIDTH), tok),
            pl.BlockSpec(cw.shape, const),
            pl.BlockSpec(wo.shape, const),
            pl.BlockSpec((1, D_MODEL), const),
            pl.BlockSpec(wr.shape, const),
            pl.BlockSpec(wrt.shape, const),
        ],
        out_specs=[
            pl.BlockSpec((t, D_MODEL), tok),
            pl.BlockSpec((t, EXT_WIDTH), tok),
            pl.BlockSpec((N_EXPERTS, t), lambda i: (0, i)),
        ],
        out_shape=[
            jax.ShapeDtypeStruct((n, D_MODEL), jnp.float32),
            jax.ShapeDtypeStruct((n, EXT_WIDTH), jnp.float32),
            jax.ShapeDtypeStruct((N_EXPERTS, n), jnp.float32),
        ],
        compiler_params=_params("arbitrary"),
        name="mix_out",
    )(x2d, gb, z, z, z, ya, cw, wo, g2, wr, wrt)


def _tri(n, strict, lower):
    r = lax.broadcasted_iota(jnp.int32, (n, n), 0)
    c = lax.broadcasted_iota(jnp.int32, (n, n), 1)
    if lower:
        m = (c < r) if strict else (c <= r)
    else:
        m = (r < c) if strict else (r <= c)
    return jnp.where(m, 1.0, 0.0).astype(jnp.bfloat16)


def _select_kernel(cap, aff_ref, idx_ref, off_ref):
    e, nr, _ = aff_ref.shape
    bits = pltpu.bitcast(aff_ref[...], jnp.int32)
    capf = jnp.float32(cap)

    def count(mask):
        return jnp.sum(jnp.where(mask, 1.0, 0.0), axis=(1, 2), keepdims=True)

    def search(_, lohi):
        lo, hi = lohi
        mid = lo + lax.shift_right_logical(hi - lo, 1)
        ge = count(bits >= mid) >= capf
        return jnp.where(ge, mid, lo), jnp.where(ge, hi, mid)

    lo0 = jnp.zeros((e, 1, 1), jnp.int32)
    hi0 = jnp.full((e, 1, 1), 0x7F800000, jnp.int32)
    thr, _ = lax.fori_loop(0, 31, search, (lo0, hi0))
    need = capf - count(bits > thr)

    upper = _tri(LANES, strict=False, lower=False)
    ones_ll = jnp.ones((LANES, LANES), jnp.bfloat16)
    ones_rr = jnp.ones((nr, nr), jnp.bfloat16)
    lower_rr = _tri(nr, strict=True, lower=True)
    upper_rr = _tri(nr, strict=False, lower=False)
    ones_8l = jnp.ones((8, LANES), jnp.bfloat16)
    ones_8r = jnp.ones((8, nr), jnp.bfloat16)
    c_col = lax.broadcasted_iota(jnp.int32, (cap, LANES), 0).astype(jnp.float32)
    c_row = lax.broadcasted_iota(jnp.int32, (cap, nr), 0).astype(jnp.float32)
    lane_r = lax.broadcasted_iota(jnp.int32, (cap, nr), 1).astype(jnp.float32)
    f32 = jnp.float32

    for x in range(e):
        bx = pltpu.bitcast(aff_ref[x], jnp.int32)
        eqf = jnp.where(bx == thr[x], 1.0, 0.0)
        eqm = eqf.astype(jnp.bfloat16)
        eq_in = jnp.dot(eqm, upper, preferred_element_type=f32)
        eq_rc = jnp.dot(eqm, ones_ll, preferred_element_type=f32).astype(jnp.bfloat16)
        eq_off = jnp.dot(lower_rr, eq_rc, preferred_element_type=f32)
        eq_rank = eq_in - eqf + eq_off
        tie = jnp.where(eq_rank < need[x], eqf, 0.0)
        m = jnp.where(bx > thr[x], 1.0, tie).astype(jnp.bfloat16)
        lcum = jnp.dot(m, upper, preferred_element_type=f32).astype(jnp.bfloat16)
        rc = jnp.dot(m, ones_ll, preferred_element_type=f32).astype(jnp.bfloat16)
        rc_l = lax.dot_general(ones_8l, m, _NT, preferred_element_type=f32)
        rowend_l = jnp.dot(rc_l.astype(jnp.bfloat16), upper_rr, preferred_element_type=f32)
        off_ref[x:x + 1, :] = (rowend_l - rc_l)[0:1].astype(jnp.int32)
        a = jnp.where(rowend_l[0:1] <= c_row, 1.0, 0.0).astype(jnp.bfloat16)
        rowoff = jnp.dot(a, rc, preferred_element_type=f32)
        rowof = jnp.dot(a, ones_rr, preferred_element_type=f32)
        onehot = jnp.where(rowof == lane_r, 1.0, 0.0).astype(jnp.bfloat16)
        lrow = jnp.dot(onehot, lcum, preferred_element_type=f32)
        b = jnp.where(lrow <= c_col - rowoff, 1.0, 0.0).astype(jnp.bfloat16)
        col_l = lax.dot_general(ones_8l, b, _NT, preferred_element_type=f32)
        row_l = lax.dot_general(ones_8r, a, _NT, preferred_element_type=f32)
        idx_ref[x:x + 1, :] = (row_l * LANES + col_l)[0:1].astype(jnp.int32)


def _select(aff3, cap):
    e, nr, _ = aff3.shape
    return pl.pallas_call(
        functools.partial(_select_kernel, cap),
        out_shape=[jax.ShapeDtypeStruct((e, cap), jnp.int32), jax.ShapeDtypeStruct((e, nr), jnp.int32)],
        compiler_params=pltpu.CompilerParams(vmem_limit_bytes=VMEM_LIMIT),
        name="ec_select",
    )(aff3)


def _ffn_kernel(idx_ref, nidx_ref, ext_hbm, wg_ref, wu_ref, wd_ref, y_ref, xbuf, sem):
    tm = xbuf.shape[1]
    e = pl.program_id(0)
    step = e * pl.num_programs(1) + pl.program_id(1)
    last = pl.num_programs(0) * pl.num_programs(1) - 1
    slot = step % 2

    def gather(rows_ref, dst):
        def issue(r, _):
            tok = rows_ref[0, 0, r]
            pltpu.make_async_copy(ext_hbm.at[pl.ds(tok, 1)], xbuf.at[dst, pl.ds(r, 1)], sem.at[dst]).start()
            return 0

        lax.fori_loop(0, tm, issue, 0, unroll=8)

    @pl.when(step == 0)
    def _():
        gather(idx_ref, 0)

    @pl.when(step < last)
    def _():
        gather(nidx_ref, 1 - slot)

    pltpu.make_async_copy(ext_hbm.at[pl.ds(0, tm)], xbuf.at[slot], sem.at[slot]).wait()
    xb = xbuf[slot, :, :D_MODEL].astype(jnp.bfloat16)
    side = xbuf[slot, :, D_MODEL:]
    lane = lax.broadcasted_iota(jnp.int32, side.shape, 1)
    gate = jnp.sum(jnp.where(lane == e, side, 0.0), axis=-1, keepdims=True)
    hg = jnp.dot(xb, wg_ref[0], preferred_element_type=jnp.float32)
    hu = jnp.dot(xb, wu_ref[0], preferred_element_type=jnp.float32)
    hh = (hg * jax.nn.sigmoid(hg) * hu).astype(jnp.bfloat16)
    y = jnp.dot(hh, wd_ref[0], preferred_element_type=jnp.float32) * gate
    y_ref[:, :D_MODEL] = y.astype(jnp.bfloat16)
    keep = (lane == TOK_HI_LANE) | (lane == TOK_LO_LANE)
    y_ref[:, D_MODEL:] = jnp.where(keep, side, 0.0).astype(jnp.bfloat16)


def _ffn(idx, ext, wg, wu, wd):
    e, cap = idx.shape
    tm = min(FFN_TM, cap)
    steps = cap // tm
    idx3 = idx.reshape(e * steps, 1, tm)
    return pl.pallas_call(
        _ffn_kernel,
        grid=(e, steps),
        in_specs=[
            pl.BlockSpec((1, 1, tm), lambda x, j: (x * steps + j, 0, 0), memory_space=pltpu.SMEM),
            pl.BlockSpec((1, 1, tm), lambda x, j: (jnp.minimum(x * steps + j + 1, e * steps - 1), 0, 0),
                         memory_space=pltpu.SMEM),
            pl.BlockSpec(memory_space=pl.ANY),
            pl.BlockSpec((1, D_MODEL, EXPERT_FF), lambda x, j: (x, 0, 0)),
            pl.BlockSpec((1, D_MODEL, EXPERT_FF), lambda x, j: (x, 0, 0)),
            pl.BlockSpec((1, EXPERT_FF, D_MODEL), lambda x, j: (x, 0, 0)),
        ],
        out_specs=pl.BlockSpec((tm, EXT_WIDTH), lambda x, j: (x * steps + j, 0)),
        out_shape=jax.ShapeDtypeStruct((e * cap, EXT_WIDTH), jnp.bfloat16),
        scratch_shapes=[pltpu.VMEM((2, tm, EXT_WIDTH), jnp.float32), pltpu.SemaphoreType.DMA((2,))],
        compiler_params=_params("arbitrary", "arbitrary"),
        name="expert_ffn",
    )(idx3, idx3, ext, wg, wu, wd)


def _combine_kernel(cap, final, off_ref, x1_ref, y_hbm, fg_ref, o_ref, rbuf, sem):
    j = pl.program_id(0)
    t = x1_ref.shape[0]
    rows_per_tile = t // LANES
    chunk = BF16_SUBLANES

    @pl.when(j == 0)
    def _():
        rbuf[...] = jnp.zeros_like(rbuf)

    def chunk_copy(src_row, dst_row):
        return pltpu.make_async_copy(y_hbm.at[pl.ds(src_row, chunk)], rbuf.at[pl.ds(dst_row, chunk)], sem)

    pos = 0
    for x in range(N_EXPERTS):
        s = off_ref[x, j * rows_per_tile]
        end = off_ref[x, (j + 1) * rows_per_tile]
        a = (s // chunk) * chunk
        nch = jnp.where(end > s, (end - a + chunk - 1) // chunk, 0)

        def issue(c, _, a=a, pos=pos, x=x):
            chunk_copy(pl.multiple_of(x * cap + a + c * chunk, chunk),
                       pl.multiple_of(pos + c * chunk, chunk)).start()
            return 0

        lax.fori_loop(0, nch, issue, 0)
        pos = pos + nch * chunk

    def wait(c, _):
        chunk_copy(0, 0).wait()
        return 0

    lax.fori_loop(0, pos // chunk, wait, 0)

    base = (j * t).astype(jnp.float32)
    lane8 = lax.broadcasted_iota(jnp.int32, (8, LANES), 1)
    tok_w = jnp.where(lane8 == TOK_HI_LANE, float(LANES),
                      jnp.where(lane8 == TOK_LO_LANE, 1.0, 0.0)).astype(jnp.bfloat16)

    def scatter(r0, nrows):
        rows = rbuf[pl.ds(r0, nrows), :]
        tok = lax.dot_general(tok_w, rows[:, D_MODEL:], _NT, preferred_element_type=jnp.float32)[0:1]
        col = lax.broadcasted_iota(jnp.int32, (1, nrows), 1) + r0
        tok = jnp.where(col < pos, tok - base, -1.0)
        sub = lax.broadcasted_iota(jnp.int32, (t, nrows), 0).astype(jnp.float32)
        onehot = jnp.where(tok == sub, 1.0, 0.0).astype(jnp.bfloat16)
        return jnp.dot(onehot, rows[:, :D_MODEL], preferred_element_type=jnp.float32)

    o_ref[...] = x1_ref[...] + scatter(0, COMBINE_MAIN)

    def extra(b, _):
        o_ref[...] += scatter(pl.multiple_of(b * COMBINE_BLK, COMBINE_BLK), COMBINE_BLK)
        return 0

    lax.fori_loop(COMBINE_MAIN // COMBINE_BLK, (pos + COMBINE_BLK - 1) // COMBINE_BLK, extra, 0)
    if final:
        o_ref[...] = _rms(o_ref[...], fg_ref[...])


def _combine(offs, x1, yext, fg, cap, final):
    n = x1.shape[0]
    t = min(COMBINE_TILE, n)
    rmax = N_EXPERTS * (t + 2 * BF16_SUBLANES)
    rmax = max(COMBINE_MAIN, ((rmax + COMBINE_BLK - 1) // COMBINE_BLK) * COMBINE_BLK)
    return pl.pallas_call(
        functools.partial(_combine_kernel, cap, final),
        grid_spec=pltpu.PrefetchScalarGridSpec(
            num_scalar_prefetch=1,
            grid=(n // t,),
            in_specs=[
                pl.BlockSpec((t, D_MODEL), lambda j, off: (j, 0)),
                pl.BlockSpec(memory_space=pl.ANY),
                pl.BlockSpec((1, D_MODEL), lambda j, off: (0, 0)),
            ],
            out_specs=pl.BlockSpec((t, D_MODEL), lambda j, off: (j, 0)),
            scratch_shapes=[pltpu.VMEM((rmax, EXT_WIDTH), jnp.bfloat16), pltpu.SemaphoreType.DMA],
        ),
        out_shape=jax.ShapeDtypeStruct((n, D_MODEL), jnp.float32),
        compiler_params=_params("arbitrary"),
        name="ec_combine",
    )(offs, x1, yext, fg)


def _rope_tables(seq):
    pos = jnp.arange(seq, dtype=jnp.float32)
    inv_freq = ROPE_THETA ** (-jnp.arange(0, ROT_DIM, 2, dtype=jnp.float32) / ROT_DIM)
    ang = pos[:, None] * inv_freq[None, :]
    cos, sin = jnp.cos(ang), jnp.sin(ang)
    ones = jnp.ones((seq, HEAD_DIM - ROT_DIM), jnp.float32)
    zeros = jnp.zeros((seq, HEAD_DIM - ROT_DIM), jnp.float32)
    z8 = jnp.zeros((seq, ROT_HALF), jnp.float32)
    rep = LANES // HEAD_DIM
    ck = jnp.tile(jnp.concatenate([cos, cos, ones], axis=1), (1, rep))
    sk1 = jnp.tile(jnp.concatenate([-sin, z8, zeros], axis=1), (1, rep))
    sk2 = jnp.tile(jnp.concatenate([z8, sin, zeros], axis=1), (1, rep))
    return ck, sk1, sk2, cos.T, sin.T


def _layer_weights(l, w_in, w_out, w_router, w_gate, w_up, w_down):
    bf = jnp.bfloat16
    c3 = 3 * CONV_WIDTH
    wi = w_in[l]
    wn = jnp.concatenate([wi[:, :c3], wi[:, c3 + QK_WIDTH:c3 + 2 * QK_WIDTH]], axis=1).astype(bf)
    wq = wi[:, c3:c3 + QK_WIDTH] * (HEAD_DIM ** -0.5 * math.log2(math.e))
    wt = jnp.concatenate([wq, wi[:, c3 + 2 * QK_WIDTH:]], axis=1).T.astype(bf)
    wr = jnp.pad(w_router[l], ((0, 0), (0, LANES - N_EXPERTS))).astype(bf)
    wrt = w_router[l].T.astype(bf)
    return wn, wt, w_out[l].astype(bf), wr, wrt, w_gate[l].astype(bf), w_up[l].astype(bf), w_down[l].astype(bf)


def _encoder(x, weights, norm1_g, conv_w, lam_q1, lam_k1, lam_q2, lam_k2, subln_g, norm2_g, final_g):
    batch, seq, _ = x.shape
    n = batch * seq
    cap = EC_FACTOR * n // N_EXPERTS
    assert seq % LANES == 0 and cap % BF16_SUBLANES == 0
    rope = _rope_tables(seq)
    fg = final_g.reshape(1, D_MODEL)
    xf = x.reshape(n, D_MODEL)
    for l in range(DEPTH):
        lam_init = 0.8 - 0.6 * math.exp(-0.3 * l)
        wn, wt, wo, wr, wrt, wg, wu, wd = weights[l]
        gb, z, k, qt, vt = _proj(xf, norm1_g[l].reshape(1, D_MODEL), wn, wt, rope, batch, seq)
        ya = _attention(qt, k.reshape(batch, seq, QK_WIDTH), vt,
                        lam_q1[l].reshape(1, HEAD_DIM), lam_k1[l].reshape(1, HEAD_DIM),
                        lam_q2[l].reshape(1, HEAD_DIM), lam_k2[l].reshape(1, HEAD_DIM),
                        subln_g[l].reshape(V_DIM, 1), lam_init)
        x1, ext, afft = _mix(xf, gb, z, ya.reshape(n, ATTN_WIDTH), conv_w[l], wo,
                             norm2_g[l].reshape(1, D_MODEL), wr, wrt, seq)
        idx, offs = _select(afft.reshape(N_EXPERTS, n // LANES, LANES), cap)
        yext = _ffn(idx, ext, wg, wu, wd)
        offs = jnp.concatenate([offs, jnp.full((N_EXPERTS, 1), cap, jnp.int32)], axis=1)
        xf = _combine(offs, x1, yext, fg, cap, final=(l == DEPTH - 1))
    return xf.reshape(batch, seq, D_MODEL)


def kernel(x_prompt, x_sample, norm1_g, w_in, conv_w, lam_q1, lam_k1, lam_q2, lam_k2, subln_g, w_out,
           norm2_g, w_router, w_gate, w_up, w_down, final_g):
    weights = [_layer_weights(l, w_in, w_out, w_router, w_gate, w_up, w_down) for l in range(DEPTH)]
    args = (weights, norm1_g, conv_w, lam_q1, lam_k1, lam_q2, lam_k2, subln_g, norm2_g, final_g)
    return _encoder(x_prompt, *args), _encoder(x_sample, *args)
```
